```python
import math
import jax, jax.numpy as jnp
from jax import lax
import numpy as np

D_MODEL = 1024
BATCH = 16
SEQ = 4096
DEPTH = 1

MEM_LEN = 256
EPS = 1e-6
GM_WIDTH = D_MODEL
GM_CHUNK = 128
GM_GROUP_CH = 128
GM_GROUPS = GM_WIDTH // GM_GROUP_CH
S5_WIDTH = D_MODEL // 2
S5_GROUP_CH = 16
S5_GROUPS = S5_WIDTH // S5_GROUP_CH
S5_STATE = 64
CA_HEADS = 4
CA_HEAD_DIM = D_MODEL // CA_HEADS
FFN_HIDDEN = ((8 * D_MODEL + 3 * 256 - 1) // (3 * 256)) * 256
IN_COLS = 2 * GM_WIDTH + S5_WIDTH + 2 * D_MODEL
SPLIT_POINTS = (2 * GM_WIDTH, 2 * GM_WIDTH + S5_WIDTH, 2 * GM_WIDTH + S5_WIDTH + D_MODEL)

kernel_name = 'hybrid_gmlp_s5_memxattn_block'


def rms_norm(x, g):
    xf = x.astype(jnp.float32)
    y = xf * lax.rsqrt(jnp.mean(xf * xf, axis=-1, keepdims=True) + EPS)
    return (y * g.astype(jnp.float32)).astype(x.dtype)


def layer_norm(x, g, b):
    xf = x.astype(jnp.float32)
    mu = jnp.mean(xf, axis=-1, keepdims=True)
    xc = xf - mu
    y = xc * lax.rsqrt(jnp.mean(xc * xc, axis=-1, keepdims=True) + EPS)
    return (y * g.astype(jnp.float32) + b.astype(jnp.float32)).astype(x.dtype)


def gmlp_branch(z, ln_g, ln_b, w_s, b_s):
    bsz, seq, _ = z.shape
    z = jax.nn.gelu(z)
    u, v = jnp.split(z, 2, axis=-1)
    v = layer_norm(v, ln_g, ln_b)
    v = v.reshape(bsz, seq // GM_CHUNK, GM_CHUNK, GM_GROUPS, GM_GROUP_CH)
    mask = jnp.tril(jnp.ones((GM_CHUNK, GM_CHUNK), dtype=bool))
    w = jnp.where(mask[None], w_s, jnp.zeros((), w_s.dtype))
    sv = jnp.einsum('gts,bnsgc->bntgc', w, v) + b_s.T[:, :, None]
    return u * sv.reshape(bsz, seq, GM_WIDTH)


def _ssm_combine(e1, e2):
    a1r, a1i, b1r, b1i = e1
    a2r, a2i, b2r, b2i = e2
    ar = a2r * a1r - a2i * a1i
    ai = a2r * a1i + a2i * a1r
    br = a2r * b1r - a2i * b1i + b2r
    bi = a2r * b1i + a2i * b1r + b2i
    return (ar, ai, br, bi)


def s5_branch(u, lam_re, lam_im, log_step, b_re, b_im, c_re, c_im, d, w_glu):
    f32 = jnp.float32
    dt = u.dtype
    bsz, seq, _ = u.shape
    uf = u.astype(f32).reshape(bsz, seq, S5_GROUPS, S5_GROUP_CH)
    lr = lam_re.astype(f32)
    li = lam_im.astype(f32)
    step = jnp.exp(log_step.astype(f32))[:, None]
    mag = jnp.exp(lr * step)
    ab_re = mag * jnp.cos(li * step)
    ab_im = mag * jnp.sin(li * step)
    den = lr * lr + li * li
    nr = ab_re - 1.0
    co_re = (nr * lr + ab_im * li) / den
    co_im = (ab_im * lr - nr * li) / den
    br_ = b_re.astype(f32)
    bi_ = b_im.astype(f32)
    bb_re = co_re[..., None] * br_ - co_im[..., None] * bi_
    bb_im = co_re[..., None] * bi_ + co_im[..., None] * br_
    bu_re = jnp.einsum('bsgh,gph->bsgp', uf, bb_re)
    bu_im = jnp.einsum('bsgh,gph->bsgp', uf, bb_im)
    a_re = jnp.broadcast_to(ab_re, (seq, S5_GROUPS, S5_STATE))
    a_im = jnp.broadcast_to(ab_im, (seq, S5_GROUPS, S5_STATE))

    def scan_one(br, bi):
        _, _, sr, si = lax.associative_scan(_ssm_combine, (a_re, a_im, br, bi), axis=0)
        return sr, si

    s_re, s_im = jax.vmap(scan_one)(bu_re, bu_im)
    y = (jnp.einsum('bsgp,ghp->bsgh', s_re, c_re.astype(f32))
         - jnp.einsum('bsgp,ghp->bsgh', s_im, c_im.astype(f32))
         + d.astype(f32) * uf)
    y = jax.nn.gelu(y.reshape(bsz, seq, S5_WIDTH))
    y = y * jax.nn.sigmoid(y @ w_glu.astype(f32))
    return y.astype(dt)


def cross_attention(h, mem_n, w_q, w_kv, w_o):
    bsz, seq, _ = h.shape
    q = (h @ w_q).reshape(bsz, seq, CA_HEADS, CA_HEAD_DIM)
    k, v = jnp.split(mem_n @ w_kv, 2, axis=-1)
    k = k.reshape(bsz, -1, CA_HEADS, CA_HEAD_DIM)
    v = v.reshape(bsz, -1, CA_HEADS, CA_HEAD_DIM)
    s = jnp.einsum('bshd,bmhd->bhsm', q, k).astype(jnp.float32) * (CA_HEAD_DIM ** -0.5)
    p = jax.nn.softmax(s, axis=-1).astype(h.dtype)
    o = jnp.einsum('bhsm,bmhd->bshd', p, v).reshape(bsz, seq, D_MODEL)
    return o @ w_o


def swiglu(h, w_gu, w_down):
    g, u = jnp.split(h @ w_gu, 2, axis=-1)
    return (jax.nn.silu(g) * u) @ w_down


def setup_inputs(seed: int = 0) -> dict:
    key = jax.random.key(seed)
    ks = jax.random.split(key, 32)
    L = DEPTH
    f32 = jnp.float32

    def nrm(k, shape, scale):
        return jax.random.normal(k, shape, f32) * scale

    def gain(k, shape):
        return 1.0 + 0.01 * jax.random.normal(k, shape, f32)

    lam_re = -0.5 * jnp.exp(0.05 * jax.random.normal(ks[8], (L, S5_GROUPS, S5_STATE), f32))
    lam_im = (math.pi * jnp.arange(S5_STATE, dtype=f32))[None, None, :] + 0.01 * jax.random.normal(ks[9], (L, S5_GROUPS, S5_STATE), f32)
    log_step = jax.random.uniform(ks[10], (L, S5_GROUPS), f32, math.log(1e-3), math.log(1e-1))
    return {
        'x': jax.random.normal(ks[0], (BATCH, SEQ, D_MODEL), f32),
        'mem': jax.random.normal(ks[1], (BATCH, MEM_LEN, D_MODEL), f32),
        'g_mix_pre': gain(ks[2], (L, D_MODEL)),
        'w_in': nrm(ks[3], (L, D_MODEL, IN_COLS), D_MODEL ** -0.5),
        'gm_ln_g': gain(ks[4], (L, GM_WIDTH)),
        'gm_ln_b': nrm(ks[5], (L, GM_WIDTH), 0.01),
        'gm_w_s': nrm(ks[6], (L, GM_GROUPS, GM_CHUNK, GM_CHUNK), GM_CHUNK ** -0.5),
        'gm_b_s': gain(ks[7], (L, GM_GROUPS, GM_CHUNK)),
        's5_lam_re': lam_re,
        's5_lam_im': lam_im,
        's5_log_step': log_step,
        's5_b_re': nrm(ks[11], (L, S5_GROUPS, S5_STATE, S5_GROUP_CH), (2 * S5_GROUP_CH) ** -0.5),
        's5_b_im': nrm(ks[12], (L, S5_GROUPS, S5_STATE, S5_GROUP_CH), (2 * S5_GROUP_CH) ** -0.5),
        's5_c_re': nrm(ks[13], (L, S5_GROUPS, S5_GROUP_CH, S5_STATE), (2 * S5_STATE) ** -0.5),
        's5_c_im': nrm(ks[14], (L, S5_GROUPS, S5_GROUP_CH, S5_STATE), (2 * S5_STATE) ** -0.5),
        's5_d': nrm(ks[15], (L, S5_GROUPS, S5_GROUP_CH), 1.0),
        's5_w_glu': nrm(ks[16], (L, S5_WIDTH, S5_WIDTH), S5_WIDTH ** -0.5),
        'w_br_gm': nrm(ks[17], (L, GM_WIDTH, D_MODEL), GM_WIDTH ** -0.5),
        'w_br_s5': nrm(ks[18], (L, S5_WIDTH, D_MODEL), S5_WIDTH ** -0.5),
        'w_mix_out': nrm(ks[19], (L, D_MODEL, D_MODEL), D_MODEL ** -0.5),
        'g_mix_post': gain(ks[20], (L, D_MODEL)),
        'g_ca_pre': gain(ks[21], (L, D_MODEL)),
        'g_mem': gain(ks[22], (L, D_MODEL)),
        'ca_w_q': nrm(ks[23], (L, D_MODEL, D_MODEL), D_MODEL ** -0.5),
        'ca_w_kv': nrm(ks[24], (L, D_MODEL, 2 * D_MODEL), D_MODEL ** -0.5),
        'ca_w_o': nrm(ks[25], (L, D_MODEL, D_MODEL), D_MODEL ** -0.5),
        'g_ca_post': gain(ks[26], (L, D_MODEL)),
        'g_ffn_pre': gain(ks[27], (L, D_MODEL)),
        'ffn_w_gu': nrm(ks[28], (L, D_MODEL, 2 * FFN_HIDDEN), D_MODEL ** -0.5),
        'ffn_w_down': nrm(ks[29], (L, FFN_HIDDEN, D_MODEL), FFN_HIDDEN ** -0.5),
        'g_ffn_post': gain(ks[30], (L, D_MODEL)),
    }


def reference(x, mem, g_mix_pre, w_in, gm_ln_g, gm_ln_b, gm_w_s, gm_b_s,
              s5_lam_re, s5_lam_im, s5_log_step, s5_b_re, s5_b_im, s5_c_re, s5_c_im,
              s5_d, s5_w_glu, w_br_gm, w_br_s5, w_mix_out, g_mix_post,
              g_ca_pre, g_mem, ca_w_q, ca_w_kv, ca_w_o, g_ca_post,
              g_ffn_pre, ffn_w_gu, ffn_w_down, g_ffn_post):
    for l in range(DEPTH):
        h = rms_norm(x, g_mix_pre[l])
        z = h @ w_in[l]
        z_gm, z_s5, z_ga, z_gb = jnp.split(z, SPLIT_POINTS, axis=-1)
        y_gm = gmlp_branch(z_gm, gm_ln_g[l], gm_ln_b[l], gm_w_s[l], gm_b_s[l])
        y_s5 = s5_branch(z_s5, s5_lam_re[l], s5_lam_im[l], s5_log_step[l],
                         s5_b_re[l], s5_b_im[l], s5_c_re[l], s5_c_im[l], s5_d[l], s5_w_glu[l])
        merged = (jax.nn.sigmoid(z_ga) * (y_gm @ w_br_gm[l])
                  + jax.nn.sigmoid(z_gb) * (y_s5 @ w_br_s5[l]))
        x = x + rms_norm(merged @ w_mix_out[l], g_mix_post[l])
        hc = rms_norm(x, g_ca_pre[l])
        mem_n = rms_norm(mem, g_mem[l])
        x = x + rms_norm(cross_attention(hc, mem_n, ca_w_q[l], ca_w_kv[l], ca_w_o[l]), g_ca_post[l])
        hf = rms_norm(x, g_ffn_pre[l])
        x = x + rms_norm(swiglu(hf, ffn_w_gu[l], ffn_w_down[l]), g_ffn_post[l])
    return x
```

```python
import functools
import math

import jax
import jax.numpy as jnp
from jax import lax
from jax.experimental import pallas as pl
from jax.experimental.pallas import tpu as pltpu

F32 = jnp.float32
BF16 = jnp.bfloat16

EPS = 1e-6
D_MODEL = 1024
GM_CHUNK = 128
GM_GROUPS = 8
S5_WIDTH = 512
S5_GROUPS = 32
S5_GROUP_CH = 16
S5_STATE = 64
CA_HEADS = 4
CA_HEAD_DIM = D_MODEL // CA_HEADS

LANES = 128
S5_BLOCKS = S5_WIDTH // LANES
S5_GROUPS_PER_BLOCK = S5_GROUPS // S5_BLOCKS
S5_BLOCK_STATE = S5_GROUPS_PER_BLOCK * S5_STATE
S5_STATE_LANES = 2 * S5_GROUPS * S5_STATE
S5_UNITS = S5_GROUPS * S5_STATE // LANES

S5_TIME_TILE = 16
MIX_ROWS = 256
CA_ROWS = 512
FFN_ROWS = 512
FFN_HIDDEN_CHUNK = 256

VMEM_LIMIT_BYTES = 56 * 1024 * 1024


def _const_spec(shape):
    zeros = (0,) * len(shape)
    return pl.BlockSpec(shape, lambda *_: zeros, pipeline_mode=pl.Buffered(1))


def _rms(x, g):
    ms = jnp.mean(x * x, axis=-1, keepdims=True)
    return x * lax.rsqrt(ms + EPS) * g


def _gelu(x):
    c = math.sqrt(2.0 / math.pi)
    return x * (0.5 * (1.0 + jnp.tanh(c * (x + 0.044715 * (x * x * x)))))


def _dot(a, b):
    return jnp.dot(a, b, preferred_element_type=F32)


def _s5_prep_kernel(lr_ref, li_ref, ls_ref, bre_ref, bim_ref,
                    abre_ref, abim_ref, bbre_ref, bbim_ref):
    lr = lr_ref[...]
    li = li_ref[...]
    step = jnp.exp(ls_ref[...])
    mag = jnp.exp(lr * step)
    ab_re = mag * jnp.cos(li * step)
    ab_im = mag * jnp.sin(li * step)
    den = lr * lr + li * li
    nr = ab_re - 1.0
    co_re = (nr * lr + ab_im * li) / den
    co_im = (ab_im * lr - nr * li) / den
    abre_ref[...] = ab_re
    abim_ref[...] = ab_im
    bre = bre_ref[...]
    bim = bim_ref[...]
    cr = co_re[:, None, :]
    ci = co_im[:, None, :]
    bbre_ref[...] = cr * bre - ci * bim
    bbim_ref[...] = cr * bim + ci * bre


def _s5_prep(lam_re, lam_im, log_step, b_re, b_im):
    g, p = lam_re.shape
    h = b_re.shape[-1]
    bre_t = jnp.swapaxes(b_re, 1, 2)
    bim_t = jnp.swapaxes(b_im, 1, 2)
    return pl.pallas_call(
        _s5_prep_kernel,
        out_shape=(jax.ShapeDtypeStruct((g, p), F32), jax.ShapeDtypeStruct((g, p), F32),
                   jax.ShapeDtypeStruct((g, h, p), F32), jax.ShapeDtypeStruct((g, h, p), F32)),
        name="s5_prep",
    )(lam_re, lam_im, log_step.reshape(g, 1), bre_t, bim_t)


def _block_diag_in(bb):
    eye = jnp.eye(S5_GROUPS_PER_BLOCK, dtype=F32)
    bb4 = bb.reshape(S5_BLOCKS, S5_GROUPS_PER_BLOCK, S5_GROUP_CH, S5_STATE)
    return jnp.einsum('jghp,gk->jghkp', bb4, eye).reshape(S5_BLOCKS, LANES, S5_BLOCK_STATE)


def _block_diag_out(c):
    eye = jnp.eye(S5_GROUPS_PER_BLOCK, dtype=F32)
    c4 = c.reshape(S5_BLOCKS, S5_GROUPS_PER_BLOCK, S5_GROUP_CH, S5_STATE)
    return jnp.einsum('jghp,gk->jgpkh', c4, eye).reshape(S5_BLOCKS, S5_BLOCK_STATE, LANES)


def _s5_kernel(x_ref, g_ref, w_ref, bbd_ref, are_ref, aim_ref, cbd_ref, d_ref, wglu_ref,
               out_ref,
               state_ref, zs_ref, u_ref, bu_ref, s_ref, ys_ref, ybt_ref, *, nb, tt):
    rows = nb * tt

    @pl.when(pl.program_id(0) == 0)
    def _():
        state_ref[...] = jnp.zeros_like(state_ref)

    x = x_ref[...].reshape(rows, D_MODEL)
    h = _rms(x, g_ref[...]).astype(BF16)
    z = _dot(h, w_ref[...])
    for j in range(S5_BLOCKS):
        zs_ref[j] = z[:, j * LANES:(j + 1) * LANES]

    for t in range(tt):
        for j in range(S5_BLOCKS):
            u_ref[t * nb:(t + 1) * nb, j * LANES:(j + 1) * LANES] = (
                zs_ref[j, pl.ds(t, nb, stride=tt), :])

    blk = 2 * S5_BLOCK_STATE
    for j in range(S5_BLOCKS):
        uj = u_ref[:, j * LANES:(j + 1) * LANES].astype(BF16)
        bu_ref[:, j * blk:(j + 1) * blk] = _dot(uj, bbd_ref[j])

    for q in range(S5_UNITS):
        j, c = divmod(q, S5_UNITS // S5_BLOCKS)
        re = j * blk + c * LANES
        im = re + S5_BLOCK_STATE
        ar = jnp.broadcast_to(are_ref[q:q + 1, :], (nb, LANES))
        ai = jnp.broadcast_to(aim_ref[q:q + 1, :], (nb, LANES))
        sr = state_ref[:, re:re + LANES]
        si = state_ref[:, im:im + LANES]
        for t in range(tt):
            r0 = t * nb
            br = bu_ref[r0:r0 + nb, re:re + LANES]
            bi = bu_ref[r0:r0 + nb, im:im + LANES]
            sr, si = ar * sr - ai * si + br, ar * si + ai * sr + bi
            s_ref[r0:r0 + nb, re:re + LANES] = sr.astype(BF16)
            s_ref[r0:r0 + nb, im:im + LANES] = si.astype(BF16)
        state_ref[:, re:re + LANES] = sr
        state_ref[:, im:im + LANES] = si

    for j in range(S5_BLOCKS):
        cols = slice(j * LANES, (j + 1) * LANES)
        ys_ref[j] = (_dot(s_ref[:, j * blk:(j + 1) * blk], cbd_ref[j])
                     + d_ref[:, cols] * u_ref[:, cols])

    for b in range(nb):
        for j in range(S5_BLOCKS):
            ybt_ref[b * tt:(b + 1) * tt, j * LANES:(j + 1) * LANES] = (
                ys_ref[j, pl.ds(b, tt, stride=nb), :])

    y = _gelu(ybt_ref[...])
    gate = jax.nn.sigmoid(_dot(y.astype(BF16), wglu_ref[...]))
    out_ref[...] = (y * gate).astype(out_ref.dtype).reshape(nb, tt, S5_WIDTH)


def _s5_branch(x, g_pre, w_s5, bbd, a_re, a_im, cbd, d, w_glu):
    nb, seq, _ = x.shape
    tt = S5_TIME_TILE
    rows = nb * tt
    kernel = functools.partial(_s5_kernel, nb=nb, tt=tt)
    return pl.pallas_call(
        kernel,
        grid=(seq // tt,),
        in_specs=[
            pl.BlockSpec((nb, tt, D_MODEL), lambda i: (0, i, 0)),
            _const_spec(g_pre.shape), _const_spec(w_s5.shape), _const_spec(bbd.shape),
            _const_spec(a_re.shape), _const_spec(a_im.shape), _const_spec(cbd.shape),
            _const_spec(d.shape), _const_spec(w_glu.shape),
        ],
        out_specs=pl.BlockSpec((nb, tt, S5_WIDTH), lambda i: (0, i, 0)),
        out_shape=jax.ShapeDtypeStruct((nb, seq, S5_WIDTH), BF16),
        scratch_shapes=[
            pltpu.VMEM((nb, S5_STATE_LANES), F32),
            pltpu.VMEM((S5_BLOCKS, rows, LANES), F32),
            pltpu.VMEM((rows, S5_WIDTH), F32),
            pltpu.VMEM((rows, S5_STATE_LANES), F32),
            pltpu.VMEM((rows, S5_STATE_LANES), BF16),
            pltpu.VMEM((S5_BLOCKS, rows, LANES), F32),
            pltpu.VMEM((rows, S5_WIDTH), F32),
        ],
        compiler_params=pltpu.CompilerParams(
            dimension_semantics=("arbitrary",), vmem_limit_bytes=VMEM_LIMIT_BYTES),
        name="s5_branch",
    )(x, g_pre, w_s5, bbd, a_re, a_im, cbd, d, w_glu)


def _mixer_kernel(x_ref, ys5_ref, gpre_ref, wu_ref, wv_ref, lng_ref, lnb_ref, ws_ref, bias_ref,
                  wga_ref, wgb_ref, wbrgm_ref, wbrs5_ref, wmo_ref, gpost_ref,
                  out_ref,
                  wtril_ref, u_ref, vn_ref, ygm_ref, *, rows):
    first = jnp.logical_and(pl.program_id(0) == 0, pl.program_id(1) == 0)

    @pl.when(first)
    def _():
        r = lax.broadcasted_iota(jnp.int32, (GM_CHUNK, GM_CHUNK), 0)
        c = lax.broadcasted_iota(jnp.int32, (GM_CHUNK, GM_CHUNK), 1)
        for g in range(GM_GROUPS):
            wtril_ref[g] = jnp.where(c <= r, ws_ref[g], 0.0).astype(BF16)

    x = x_ref[...]
    h = _rms(x, gpre_ref[...]).astype(BF16)
    u_ref[...] = _gelu(_dot(h, wu_ref[...]))
    v = _gelu(_dot(h, wv_ref[...]))
    mu = jnp.mean(v, axis=-1, keepdims=True)
    vc = v - mu
    var = jnp.mean(vc * vc, axis=-1, keepdims=True)
    vn_ref[...] = (vc * lax.rsqrt(var + EPS) * lng_ref[...] + lnb_ref[...]).astype(BF16)

    for c in range(rows // GM_CHUNK):
        rs = slice(c * GM_CHUNK, (c + 1) * GM_CHUNK)
        for g in range(GM_GROUPS):
            cs = slice(g * LANES, (g + 1) * LANES)
            sv = _dot(wtril_ref[g], vn_ref[rs, cs]) + bias_ref[:, cs]
            ygm_ref[rs, cs] = (u_ref[rs, cs] * sv).astype(BF16)

    a = _dot(ygm_ref[...], wbrgm_ref[...])
    b = _dot(ys5_ref[...], wbrs5_ref[...])
    ga = jax.nn.sigmoid(_dot(h, wga_ref[...]))
    gb = jax.nn.sigmoid(_dot(h, wgb_ref[...]))
    m = (ga * a + gb * b).astype(BF16)
    o = _dot(m, wmo_ref[...])
    out_ref[...] = x + _rms(o, gpost_ref[...])


def _mixer(x, ys5, g_pre, w_u, w_v, ln_g, ln_b, w_s, bias, w_ga, w_gb, w_brgm, w_brs5, w_mo, g_post):
    nb, seq, _ = x.shape
    rows = MIX_ROWS
    kernel = functools.partial(_mixer_kernel, rows=rows)
    consts = (g_pre, w_u, w_v, ln_g, ln_b, w_s, bias, w_ga, w_gb, w_brgm, w_brs5, w_mo, g_post)
    return pl.pallas_call(
        kernel,
        grid=(nb, seq // rows),
        in_specs=[
            pl.BlockSpec((None, rows, D_MODEL), lambda b, i: (b, i, 0)),
            pl.BlockSpec((None, rows, S5_WIDTH), lambda b, i: (b, i, 0)),
        ] + [_const_spec(c.shape) for c in consts],
        out_specs=pl.BlockSpec((None, rows, D_MODEL), lambda b, i: (b, i, 0)),
        out_shape=jax.ShapeDtypeStruct(x.shape, F32),
        scratch_shapes=[
            pltpu.VMEM((GM_GROUPS, GM_CHUNK, GM_CHUNK), BF16),
            pltpu.VMEM((rows, D_MODEL), F32),
            pltpu.VMEM((rows, D_MODEL), BF16),
            pltpu.VMEM((rows, D_MODEL), BF16),
        ],
        compiler_params=pltpu.CompilerParams(
            dimension_semantics=("arbitrary", "arbitrary"), vmem_limit_bytes=VMEM_LIMIT_BYTES),
        name="mixer",
    )(x, ys5, *consts)


def _kv_kernel(mem_ref, g_ref, wkv_ref, k_ref, v_ref):
    mn = _rms(mem_ref[...], g_ref[...]).astype(BF16)
    kv = _dot(mn, wkv_ref[...])
    k_ref[...] = kv[:, :D_MODEL].astype(BF16)
    v_ref[...] = kv[:, D_MODEL:].astype(BF16)


def _mem_kv(mem, g_mem, w_kv):
    nb, mlen, _ = mem.shape
    spec = pl.BlockSpec((None, mlen, D_MODEL), lambda b: (b, 0, 0))
    return pl.pallas_call(
        _kv_kernel,
        grid=(nb,),
        in_specs=[spec, _const_spec(g_mem.shape), _const_spec(w_kv.shape)],
        out_specs=(spec, spec),
        out_shape=(jax.ShapeDtypeStruct(mem.shape, BF16), jax.ShapeDtypeStruct(mem.shape, BF16)),
        compiler_params=pltpu.CompilerParams(
            dimension_semantics=("arbitrary",), vmem_limit_bytes=VMEM_LIMIT_BYTES),
        name="mem_kv",
    )(mem, g_mem, w_kv)


def _ca_kernel(x_ref, k_ref, v_ref, gpre_ref, wq_ref, wo_ref, gpost_ref, out_ref, o_ref):
    x = x_ref[...]
    hc = _rms(x, gpre_ref[...]).astype(BF16)
    q = (_dot(hc, wq_ref[...]) * (CA_HEAD_DIM ** -0.5)).astype(BF16)
    for hd in range(CA_HEADS):
        cs = slice(hd * CA_HEAD_DIM, (hd + 1) * CA_HEAD_DIM)
        s = lax.dot_general(q[:, cs], k_ref[:, cs], (((1,), (1,)), ((), ())),
                            preferred_element_type=F32)
        p = jnp.exp(s - jnp.max(s, axis=-1, keepdims=True))
        l = jnp.sum(p, axis=-1, keepdims=True)
        o = _dot(p.astype(BF16), v_ref[:, cs])
        o_ref[:, cs] = (o * (1.0 / l)).astype(BF16)
    out = _dot(o_ref[...], wo_ref[...])
    out_ref[...] = x + _rms(out, gpost_ref[...])


def _cross_attn(x, k, v, g_pre, w_q, w_o, g_post):
    nb, seq, _ = x.shape
    mlen = k.shape[1]
    rows = CA_ROWS
    xspec = pl.BlockSpec((None, rows, D_MODEL), lambda b, i: (b, i, 0))
    mspec = pl.BlockSpec((None, mlen, D_MODEL), lambda b, i: (b, 0, 0))
    consts = (g_pre, w_q, w_o, g_post)
    return pl.pallas_call(
        _ca_kernel,
        grid=(nb, seq // rows),
        in_specs=[xspec, mspec, mspec] + [_const_spec(c.shape) for c in consts],
        out_specs=xspec,
        out_shape=jax.ShapeDtypeStruct(x.shape, F32),
        scratch_shapes=[pltpu.VMEM((rows, D_MODEL), BF16)],
        compiler_params=pltpu.CompilerParams(
            dimension_semantics=("arbitrary", "arbitrary"), vmem_limit_bytes=VMEM_LIMIT_BYTES),
        name="cross_attn",
    )(x, k, v, *consts)


def _ffn_kernel(x_ref, gpre_ref, wg_ref, wu_ref, wd_ref, gpost_ref, out_ref, a_ref, *, hidden):
    x = x_ref[...]
    hf = _rms(x, gpre_ref[...]).astype(BF16)
    for c0 in range(0, hidden, FFN_HIDDEN_CHUNK):
        cs = slice(c0, c0 + FFN_HIDDEN_CHUNK)
        g = _dot(hf, wg_ref[:, cs])
        u = _dot(hf, wu_ref[:, cs])
        a_ref[:, cs] = (g * jax.nn.sigmoid(g) * u).astype(BF16)
    d = _dot(a_ref[...], wd_ref[...])
    out_ref[...] = x + _rms(d, gpost_ref[...])


def _ffn(x, g_pre, w_g, w_u, w_d, g_post):
    nb, seq, _ = x.shape
    hidden = w_g.shape[1]
    rows = FFN_ROWS
    xspec = pl.BlockSpec((None, rows, D_MODEL), lambda b, i: (b, i, 0))
    consts = (g_pre, w_g, w_u, w_d, g_post)
    return pl.pallas_call(
        functools.partial(_ffn_kernel, hidden=hidden),
        grid=(nb, seq // rows),
        in_specs=[xspec] + [_const_spec(c.shape) for c in consts],
        out_specs=xspec,
        out_shape=jax.ShapeDtypeStruct(x.shape, F32),
        scratch_shapes=[pltpu.VMEM((rows, hidden), BF16)],
        compiler_params=pltpu.CompilerParams(
            dimension_semantics=("arbitrary", "arbitrary"), vmem_limit_bytes=VMEM_LIMIT_BYTES),
        name="ffn",
    )(x, *consts)


def kernel(x, mem, g_mix_pre, w_in, gm_ln_g, gm_ln_b, gm_w_s, gm_b_s, s5_lam_re, s5_lam_im, s5_log_step, s5_b_re, s5_b_im, s5_c_re, s5_c_im, s5_d, s5_w_glu, w_br_gm, w_br_s5, w_mix_out, g_mix_post, g_ca_pre, g_mem, ca_w_q, ca_w_kv, ca_w_o, g_ca_post, g_ffn_pre, ffn_w_gu, ffn_w_down, g_ffn_post):
    depth = w_in.shape[0]
    gm_w = D_MODEL
    c_u, c_v, c_s5 = 0, gm_w, 2 * gm_w
    c_ga = c_s5 + S5_WIDTH
    c_gb = c_ga + D_MODEL
    row = lambda g: g.reshape(1, -1)
    for l in range(depth):
        wi = w_in[l].astype(BF16)
        ab_re, ab_im, bb_re, bb_im = _s5_prep(s5_lam_re[l], s5_lam_im[l], s5_log_step[l],
                                              s5_b_re[l], s5_b_im[l])
        bbd = jnp.concatenate([_block_diag_in(bb_re), _block_diag_in(bb_im)], axis=-1).astype(BF16)
        cbd = jnp.concatenate([_block_diag_out(s5_c_re[l]), _block_diag_out(-s5_c_im[l])],
                              axis=1).astype(BF16)
        ys5 = _s5_branch(x, row(g_mix_pre[l]), wi[:, c_s5:c_ga], bbd,
                         ab_re.reshape(S5_UNITS, LANES), ab_im.reshape(S5_UNITS, LANES), cbd,
                         row(s5_d[l]), s5_w_glu[l].astype(BF16))
        bias = jnp.repeat(gm_b_s[l].T, LANES, axis=1)
        x = _mixer(x, ys5, row(g_mix_pre[l]), wi[:, c_u:c_v], wi[:, c_v:c_s5],
                   row(gm_ln_g[l]), row(gm_ln_b[l]), gm_w_s[l], bias,
                   wi[:, c_ga:c_gb], wi[:, c_gb:], w_br_gm[l].astype(BF16),
                   w_br_s5[l].astype(BF16), w_mix_out[l].astype(BF16), row(g_mix_post[l]))
        k, v = _mem_kv(mem, row(g_mem[l]), ca_w_kv[l].astype(BF16))
        x = _cross_attn(x, k, v, row(g_ca_pre[l]), ca_w_q[l].astype(BF16),
                        ca_w_o[l].astype(BF16), row(g_ca_post[l]))
        hidden = ffn_w_down.shape[1]
        wgu = ffn_w_gu[l].astype(BF16)
        x = _ffn(x, row(g_ffn_pre[l]), wgu[:, :hidden], wgu[:, hidden:],
                 ffn_w_down[l].astype(BF16), row(g_ffn_post[l]))
    return x
```
